```python
import math
import jax, jax.numpy as jnp
from jax import lax
import numpy as np

D_MODEL = 1024
BATCH = 4
SEQ = 4096
DEPTH = 2
DEC_BATCH = 128
DEC_SEQ = 8
PAST_LEN = 2048
PAGE_SIZE = 128

SSM_WIDTH = D_MODEL // 2
SSM_GROUP = 16
SSM_GROUPS = SSM_WIDTH // SSM_GROUP
SSM_STATE = 64
N_HEADS = 8
HEAD_DIM = 64
N_KV_HEADS = 8
ATTN_WIDTH = N_HEADS * HEAD_DIM
KV_WIDTH = N_KV_HEADS * HEAD_DIM
ROPE_DIM = HEAD_DIM // 4
ROPE_THETA = 500000.0
IDX_HEADS = 8
IDX_DIM = 64
IDX_ROPE_DIM = IDX_DIM // 4
TOPK_MAX = 256
Q_BLOCK = 128
GMLP_WIDTH = D_MODEL // 2
GMLP_GROUPS = 4
GMLP_CHUNK = 128
PEER_HEADS = 8
PEER_NKEYS = 128
PEER_EXPERTS = PEER_NKEYS * PEER_NKEYS
PEER_QDIM = 256
PEER_SUBTOP = 16
PEER_TOPK = 16
PEER_BLOCK = 256
EPS = 1e-6

IN_SIZES = (SSM_WIDTH, ATTN_WIDTH, KV_WIDTH, KV_WIDTH, IDX_HEADS * IDX_DIM, IDX_DIM, IDX_HEADS, 2 * GMLP_WIDTH, 3 * D_MODEL)
IN_WIDTH = SSM_WIDTH + ATTN_WIDTH + 2 * KV_WIDTH + IDX_HEADS * IDX_DIM + IDX_DIM + IDX_HEADS + 2 * GMLP_WIDTH + 3 * D_MODEL

kernel_name = 'hybrid_s5_dsa_gmlp_peer_step'


def rmsnorm(x, g):
    xf = x.astype(jnp.float32)
    r = lax.rsqrt(jnp.mean(xf * xf, axis=-1, keepdims=True) + EPS)
    return (xf * r).astype(x.dtype) * g


def rope_partial(x, pos, rot_dim):
    half = rot_dim // 2
    inv = ROPE_THETA ** (-jnp.arange(half, dtype=jnp.float32) / half)
    ang = pos.astype(jnp.float32)[:, None] * inv[None, :]
    cos = jnp.cos(ang)[:, None, :]
    sin = jnp.sin(ang)[:, None, :]
    xf = x.astype(jnp.float32)
    x1, x2, rest = xf[..., :half], xf[..., half:rot_dim], xf[..., rot_dim:]
    return jnp.concatenate([x1 * cos - x2 * sin, x2 * cos + x1 * sin, rest], axis=-1).astype(x.dtype)


def s5_scan(u, h0_re, h0_im, a_re, a_im, log_step, b_re, b_im, c_re, c_im, d):
    f32 = jnp.float32
    bsz, L, _ = u.shape
    uf = u.astype(f32)
    ug = uf.reshape(bsz, L, SSM_GROUPS, SSM_GROUP)
    dt = jnp.exp(log_step.astype(f32))[:, None]
    ar, ai = a_re.astype(f32), a_im.astype(f32)
    mag = jnp.exp(dt * ar)
    abar_re = mag * jnp.cos(dt * ai)
    abar_im = mag * jnp.sin(dt * ai)
    den = ar * ar + ai * ai
    nr = abar_re - 1.0
    coef_re = (nr * ar + abar_im * ai) / den
    coef_im = (abar_im * ar - nr * ai) / den
    br, bi = b_re.astype(f32), b_im.astype(f32)
    bb_re = coef_re[..., None] * br - coef_im[..., None] * bi
    bb_im = coef_re[..., None] * bi + coef_im[..., None] * br
    bu_re = jnp.einsum('gpc,blgc->blgp', bb_re, ug)
    bu_im = jnp.einsum('gpc,blgc->blgp', bb_im, ug)
    hr, hi = h0_re.astype(f32), h0_im.astype(f32)
    bu_re = bu_re.at[:, 0].add(abar_re * hr - abar_im * hi)
    bu_im = bu_im.at[:, 0].add(abar_re * hi + abar_im * hr)
    a_l_re = jnp.broadcast_to(abar_re, bu_re.shape)
    a_l_im = jnp.broadcast_to(abar_im, bu_im.shape)

    def combine(e1, e2):
        a1r, a1i, b1r, b1i = e1
        a2r, a2i, b2r, b2i = e2
        return (a2r * a1r - a2i * a1i, a2r * a1i + a2i * a1r,
                a2r * b1r - a2i * b1i + b2r, a2r * b1i + a2i * b1r + b2i)

    _, _, h_re, h_im = lax.associative_scan(combine, (a_l_re, a_l_im, bu_re, bu_im), axis=1)
    y = jnp.einsum('gcp,blgp->blgc', c_re.astype(f32), h_re) - jnp.einsum('gcp,blgp->blgc', c_im.astype(f32), h_im)
    y = y.reshape(bsz, L, SSM_WIDTH) + d.astype(f32) * uf
    return y.astype(u.dtype), h_re[:, -1], h_im[:, -1]


def indexer_scores(q_idx, w_idx, k_idx):
    s = jnp.einsum('bthd,bld->bthl', q_idx.astype(jnp.float32), k_idx.astype(jnp.float32)) * (IDX_DIM ** -0.5)
    w = w_idx.astype(jnp.float32) * (IDX_HEADS ** -0.5)
    return jnp.einsum('bth,bthl->btl', w, jax.nn.relu(s))


def select_topk(score, valid, k):
    score = jnp.where(valid, score, -jnp.inf)
    top_s, top_i = lax.top_k(score, k)
    return top_i, jnp.isfinite(top_s)


def sparse_attend(q, k_sel, v_sel, valid):
    s = jnp.einsum('bthd,btkhd->bthk', q.astype(jnp.float32), k_sel.astype(jnp.float32)) * (HEAD_DIM ** -0.5)
    s = jnp.where(valid[:, :, None, :], s, -jnp.inf)
    p = jax.nn.softmax(s, axis=-1)
    return jnp.einsum('bthk,btkhd->bthd', p, v_sel.astype(jnp.float32)).astype(q.dtype)


def dsa_prompt(q, k, v, q_idx, k_idx, w_idx):
    bsz, L = q.shape[0], q.shape[1]
    topk = min(TOPK_MAX, L // 4)
    nblk = L // Q_BLOCK
    key_pos = jnp.arange(L)

    def to_blocks(a):
        return jnp.moveaxis(a.reshape((bsz, nblk, Q_BLOCK) + a.shape[2:]), 1, 0)

    def blk(args):
        qb, qib, wb, t0 = args
        sc = indexer_scores(qib, wb, k_idx)
        qpos = t0 + jnp.arange(Q_BLOCK)
        valid = key_pos[None, :] <= qpos[:, None]
        idx, ok = select_topk(sc, valid[None], topk)
        ks = jax.vmap(lambda kk, ii: kk[ii])(k, idx)
        vs = jax.vmap(lambda vv, ii: vv[ii])(v, idx)
        return sparse_attend(qb, ks, vs, ok)

    t0s = jnp.arange(nblk) * Q_BLOCK
    out = lax.map(blk, (to_blocks(q), to_blocks(q_idx), to_blocks(w_idx), t0s))
    return jnp.moveaxis(out, 0, 1).reshape(bsz, L, N_HEADS, HEAD_DIM)


def dsa_sample(q, k, v, q_idx, k_idx, w_idx, cache_k, cache_v, cache_kidx, page_table):
    bd, T = q.shape[0], q.shape[1]
    n_pages = PAST_LEN // PAGE_SIZE
    L = PAST_LEN + T
    topk = min(TOPK_MAX, L // 4)
    kidx_past = cache_kidx[page_table].reshape(bd, n_pages * PAGE_SIZE, IDX_DIM)
    kidx_all = jnp.concatenate([kidx_past.astype(k_idx.dtype), k_idx], axis=1)
    flat_k = cache_k.reshape(-1, N_KV_HEADS, HEAD_DIM)
    flat_v = cache_v.reshape(-1, N_KV_HEADS, HEAD_DIM)
    key_pos = jnp.arange(L)

    def one(args):
        qt, qit, wt, t = args
        sc = indexer_scores(qit[:, None], wt[:, None], kidx_all)[:, 0]
        valid = key_pos <= PAST_LEN + t
        idx, ok = select_topk(sc, valid[None], topk)
        from_past = idx < PAST_LEN
        pidx = jnp.minimum(idx, PAST_LEN - 1)
        phys = jnp.take_along_axis(page_table, pidx // PAGE_SIZE, axis=1) * PAGE_SIZE + pidx % PAGE_SIZE
        nidx = jnp.clip(idx - PAST_LEN, 0, T - 1)

        def gather(flat, new):
            past_rows = flat[phys].astype(new.dtype)
            new_rows = jax.vmap(lambda nn, ii: nn[ii])(new, nidx)
            return jnp.where(from_past[..., None, None], past_rows, new_rows)

        ks = gather(flat_k, k)
        vs = gather(flat_v, v)
        return sparse_attend(qt[:, None], ks[:, None], vs[:, None], ok[:, None])[:, 0]

    xs = (jnp.moveaxis(q, 1, 0), jnp.moveaxis(q_idx, 1, 0), jnp.moveaxis(w_idx, 1, 0), jnp.arange(T))
    out = lax.map(one, xs)
    return jnp.moveaxis(out, 0, 1)


def gmlp_branch(z, ln_g, ln_b, ws, bs):
    u, v = jnp.split(z, 2, axis=-1)
    vf = v.astype(jnp.float32)
    mu = jnp.mean(vf, axis=-1, keepdims=True)
    var = jnp.mean(jnp.square(vf - mu), axis=-1, keepdims=True)
    vn = ((vf - mu) * lax.rsqrt(var + EPS)).astype(v.dtype) * ln_g + ln_b
    bsz, L, _ = v.shape
    c = min(GMLP_CHUNK, L)
    mask = jnp.tril(jnp.ones((c, c), dtype=bool))
    w = jnp.where(mask[None], ws[:, :c, :c], 0.0)
    vb = vn.reshape(bsz, L // c, c, GMLP_GROUPS, GMLP_WIDTH // GMLP_GROUPS)
    mixed = jnp.einsum('gts,bnsgd->bntgd', w, vb) + bs[:, :c].T[None, None, :, :, None]
    return u * mixed.reshape(bsz, L, GMLP_WIDTH), vn


def peer(x, wq, keys, u_tab, v_tab):
    shp = x.shape
    xt = x.reshape(-1, D_MODEL)
    n = xt.shape[0]
    nb = -(-n // PEER_BLOCK)
    xt = jnp.pad(xt, ((0, nb * PEER_BLOCK - n), (0, 0))).reshape(nb, PEER_BLOCK, D_MODEL)

    def blk(xb):
        q = (xb @ wq).reshape(PEER_BLOCK, PEER_HEADS, 2, PEER_QDIM // 2)
        s = jnp.einsum('thcd,hcnd->thcn', q.astype(jnp.float32), keys.astype(jnp.float32))
        ts, ti = lax.top_k(s, PEER_SUBTOP)
        cand = (ts[:, :, 0, :, None] + ts[:, :, 1, None, :]).reshape(PEER_BLOCK, PEER_HEADS, -1)
        cidx = (ti[:, :, 0, :, None] * PEER_NKEYS + ti[:, :, 1, None, :]).reshape(PEER_BLOCK, PEER_HEADS, -1)
        fs, fi = lax.top_k(cand, PEER_TOPK)
        e = jnp.take_along_axis(cidx, fi, axis=-1)
        g = jax.nn.softmax(fs, axis=-1)
        act = jax.nn.gelu(jnp.einsum('thkd,td->thk', u_tab[e], xb).astype(jnp.float32))
        return jnp.einsum('thk,thkd->td', (g * act).astype(xb.dtype), v_tab[e])

    y = lax.map(blk, xt).reshape(-1, D_MODEL)[:n]
    return y.reshape(shp)


def trunk_layer(x, pos, h0_re, h0_im, attend, p):
    bsz, L, _ = x.shape
    xn = rmsnorm(x, p['norm_mix'])
    z = xn @ p['w_in']
    points = [int(c) for c in np.cumsum(IN_SIZES)[:-1]]
    u_ssm, q, k, v, qi, ki, wi, zg, gates = jnp.split(z, points, axis=-1)
    y_ssm, h_re, h_im = s5_scan(u_ssm, h0_re, h0_im, p['ssm_a_re'], p['ssm_a_im'], p['ssm_log_step'],
                                p['ssm_b_re'], p['ssm_b_im'], p['ssm_c_re'], p['ssm_c_im'], p['ssm_d'])
    ga = jax.nn.gelu(y_ssm) @ p['w_glu']
    br_a = ga[..., :D_MODEL] * jax.nn.sigmoid(ga[..., D_MODEL:])
    q = rope_partial(q.reshape(bsz, L, N_HEADS, HEAD_DIM), pos, ROPE_DIM)
    k = rope_partial(k.reshape(bsz, L, N_KV_HEADS, HEAD_DIM), pos, ROPE_DIM)
    v = v.reshape(bsz, L, N_KV_HEADS, HEAD_DIM)
    qi = rope_partial(qi.reshape(bsz, L, IDX_HEADS, IDX_DIM), pos, IDX_ROPE_DIM)
    ki = rope_partial(ki[:, :, None, :], pos, IDX_ROPE_DIM)[:, :, 0]
    attn = attend(q, k, v, qi, ki, wi)
    br_b = attn.reshape(bsz, L, ATTN_WIDTH) @ p['w_attn_o']
    mixed, v_rows = gmlp_branch(jax.nn.gelu(zg), p['gmlp_ln_g'], p['gmlp_ln_b'], p['gmlp_ws'], p['gmlp_bs'])
    br_c = mixed @ p['w_gmlp_o']
    g_a, g_b, g_c = jnp.split(gates, 3, axis=-1)
    merged = jax.nn.sigmoid(g_a) * br_a + jax.nn.sigmoid(g_b) * br_b + jax.nn.sigmoid(g_c) * br_c
    x = x + merged @ p['w_out']
    x = x + peer(rmsnorm(x, p['norm_ffn']), p['peer_wq'], p['peer_keys'], p['peer_u'], p['peer_v'])
    return x, k, v, ki, h_re, h_im, v_rows


def setup_inputs(seed: int = 0) -> dict:
    key = jax.random.key(seed)
    ks = iter(jax.random.split(key, 48))
    f32 = jnp.float32

    def nrm(shape, scale):
        return jax.random.normal(next(ks), shape, f32) * scale

    n_pages = PAST_LEN // PAGE_SIZE
    n_used = DEC_BATCH * n_pages
    n_pool = n_used + max(1, n_used // 4)
    x_prompt = nrm((BATCH, SEQ, D_MODEL), 1.0)
    x_sample = nrm((DEC_BATCH, DEC_SEQ, D_MODEL), 1.0)
    cache_k = nrm((DEPTH, n_pool, PAGE_SIZE, N_KV_HEADS, HEAD_DIM), 1.0)
    cache_v = nrm((DEPTH, n_pool, PAGE_SIZE, N_KV_HEADS, HEAD_DIM), 1.0)
    cache_kidx = nrm((DEPTH, n_pool, PAGE_SIZE, IDX_DIM), 1.0)
    page_table = jax.random.permutation(next(ks), n_pool)[:n_used].reshape(DEC_BATCH, n_pages).astype(jnp.int32)
    state_ssm_re = nrm((DEPTH, DEC_BATCH, SSM_GROUPS, SSM_STATE), 0.5)
    state_ssm_im = nrm((DEPTH, DEC_BATCH, SSM_GROUPS, SSM_STATE), 0.5)
    norm_mix = 1.0 + nrm((DEPTH, D_MODEL), 0.02)
    w_in = nrm((DEPTH, D_MODEL, IN_WIDTH), D_MODEL ** -0.5)
    ssm_a_re = -0.5 + nrm((DEPTH, SSM_GROUPS, SSM_STATE), 0.01)
    ssm_a_im = jnp.pi * jnp.arange(SSM_STATE, dtype=f32) + nrm((DEPTH, SSM_GROUPS, SSM_STATE), 0.01)
    ssm_log_step = jax.random.uniform(next(ks), (DEPTH, SSM_GROUPS), f32, math.log(1e-3), math.log(1e-1))
    ssm_b_re = nrm((DEPTH, SSM_GROUPS, SSM_STATE, SSM_GROUP), (0.5 / SSM_GROUP) ** 0.5)
    ssm_b_im = nrm((DEPTH, SSM_GROUPS, SSM_STATE, SSM_GROUP), (0.5 / SSM_GROUP) ** 0.5)
    ssm_c_re = nrm((DEPTH, SSM_GROUPS, SSM_GROUP, SSM_STATE), (0.5 / SSM_STATE) ** 0.5)
    ssm_c_im = nrm((DEPTH, SSM_GROUPS, SSM_GROUP, SSM_STATE), (0.5 / SSM_STATE) ** 0.5)
    ssm_d = nrm((DEPTH, SSM_WIDTH), 1.0)
    w_glu = nrm((DEPTH, SSM_WIDTH, 2 * D_MODEL), SSM_WIDTH ** -0.5)
    w_attn_o = nrm((DEPTH, ATTN_WIDTH, D_MODEL), ATTN_WIDTH ** -0.5)
    gmlp_ln_g = 1.0 + nrm((DEPTH, GMLP_WIDTH), 0.02)
    gmlp_ln_b = nrm((DEPTH, GMLP_WIDTH), 0.02)
    gmlp_ws = nrm((DEPTH, GMLP_GROUPS, GMLP_CHUNK, GMLP_CHUNK), GMLP_CHUNK ** -0.5)
    gmlp_bs = 1.0 + nrm((DEPTH, GMLP_GROUPS, GMLP_CHUNK), 0.02)
    w_gmlp_o = nrm((DEPTH, GMLP_WIDTH, D_MODEL), GMLP_WIDTH ** -0.5)
    w_out = nrm((DEPTH, D_MODEL, D_MODEL), D_MODEL ** -0.5)
    norm_ffn = 1.0 + nrm((DEPTH, D_MODEL), 0.02)
    peer_wq = nrm((DEPTH, D_MODEL, PEER_HEADS * PEER_QDIM), D_MODEL ** -0.5)
    peer_keys = nrm((DEPTH, PEER_HEADS, 2, PEER_NKEYS, PEER_QDIM // 2), (PEER_QDIM // 2) ** -0.5)
    peer_u = nrm((DEPTH, PEER_EXPERTS, D_MODEL), D_MODEL ** -0.5)
    peer_v = nrm((DEPTH, PEER_EXPERTS, D_MODEL), D_MODEL ** -0.5)
    norm_final = 1.0 + nrm((D_MODEL,), 0.02)
    return {'x_prompt': x_prompt, 'x_sample': x_sample, 'cache_k': cache_k, 'cache_v': cache_v,
            'cache_kidx': cache_kidx, 'page_table': page_table, 'state_ssm_re': state_ssm_re,
            'state_ssm_im': state_ssm_im, 'norm_mix': norm_mix, 'w_in': w_in, 'ssm_a_re': ssm_a_re,
            'ssm_a_im': ssm_a_im, 'ssm_log_step': ssm_log_step, 'ssm_b_re': ssm_b_re, 'ssm_b_im': ssm_b_im,
            'ssm_c_re': ssm_c_re, 'ssm_c_im': ssm_c_im, 'ssm_d': ssm_d, 'w_glu': w_glu, 'w_attn_o': w_attn_o,
            'gmlp_ln_g': gmlp_ln_g, 'gmlp_ln_b': gmlp_ln_b, 'gmlp_ws': gmlp_ws, 'gmlp_bs': gmlp_bs,
            'w_gmlp_o': w_gmlp_o, 'w_out': w_out, 'norm_ffn': norm_ffn, 'peer_wq': peer_wq,
            'peer_keys': peer_keys, 'peer_u': peer_u, 'peer_v': peer_v, 'norm_final': norm_final}


def reference(x_prompt, x_sample, cache_k, cache_v, cache_kidx, page_table, state_ssm_re, state_ssm_im,
              norm_mix, w_in, ssm_a_re, ssm_a_im, ssm_log_step, ssm_b_re, ssm_b_im, ssm_c_re, ssm_c_im,
              ssm_d, w_glu, w_attn_o, gmlp_ln_g, gmlp_ln_b, gmlp_ws, gmlp_bs, w_gmlp_o, w_out, norm_ffn,
              peer_wq, peer_keys, peer_u, peer_v, norm_final):
    pos_p = jnp.arange(SEQ)
    pos_s = PAST_LEN + jnp.arange(DEC_SEQ)
    h0_re_p = jnp.zeros((BATCH, SSM_GROUPS, SSM_STATE), jnp.float32)
    h0_im_p = jnp.zeros((BATCH, SSM_GROUPS, SSM_STATE), jnp.float32)
    xp, xs = x_prompt, x_sample
    kp, vp, kip, hrp, hip = [], [], [], [], []
    ks_, vs_, kis, hrs, his, gvs = [], [], [], [], [], []
    for l in range(DEPTH):
        p = dict(norm_mix=norm_mix[l], w_in=w_in[l], ssm_a_re=ssm_a_re[l], ssm_a_im=ssm_a_im[l],
                 ssm_log_step=ssm_log_step[l], ssm_b_re=ssm_b_re[l], ssm_b_im=ssm_b_im[l],
                 ssm_c_re=ssm_c_re[l], ssm_c_im=ssm_c_im[l], ssm_d=ssm_d[l], w_glu=w_glu[l],
                 w_attn_o=w_attn_o[l], gmlp_ln_g=gmlp_ln_g[l], gmlp_ln_b=gmlp_ln_b[l], gmlp_ws=gmlp_ws[l],
                 gmlp_bs=gmlp_bs[l], w_gmlp_o=w_gmlp_o[l], w_out=w_out[l], norm_ffn=norm_ffn[l],
                 peer_wq=peer_wq[l], peer_keys=peer_keys[l], peer_u=peer_u[l], peer_v=peer_v[l])
        xp, k1, v1, ki1, hr1, hi1, _gv = trunk_layer(xp, pos_p, h0_re_p, h0_im_p, dsa_prompt, p)
        kp.append(k1); vp.append(v1); kip.append(ki1); hrp.append(hr1); hip.append(hi1)
        ck, cv, cki = cache_k[l], cache_v[l], cache_kidx[l]

        def attend_s(q, k, v, qi, ki, wi, ck=ck, cv=cv, cki=cki):
            return dsa_sample(q, k, v, qi, ki, wi, ck, cv, cki, page_table)

        xs, k2, v2, ki2, hr2, hi2, gv2 = trunk_layer(xs, pos_s, state_ssm_re[l], state_ssm_im[l], attend_s, p)
        ks_.append(k2); vs_.append(v2); kis.append(ki2); hrs.append(hr2); his.append(hi2); gvs.append(gv2)
    y_prompt = rmsnorm(xp, norm_final)
    y_sample = rmsnorm(xs, norm_final)
    return (y_prompt, y_sample,
            jnp.stack(kp), jnp.stack(vp), jnp.stack(kip), jnp.stack(hrp), jnp.stack(hip),
            jnp.stack(ks_), jnp.stack(vs_), jnp.stack(kis), jnp.stack(hrs), jnp.stack(his), jnp.stack(gvs))
```

```python
import functools
import math

import jax
import jax.numpy as jnp
import numpy as np
from jax import lax
from jax.experimental import pallas as pl
from jax.experimental.pallas import tpu as pltpu

F32 = jnp.float32
BF16 = jnp.bfloat16
I32 = jnp.int32

D_MODEL = 1024
PAST_LEN = 2048
PAGE_SIZE = 128
SSM_WIDTH = 512
SSM_GROUP = 16
SSM_GROUPS = 32
SSM_STATE = 64
SSM_LANES = SSM_GROUPS * SSM_STATE
N_HEADS = 8
HEAD_DIM = 64
ATTN_WIDTH = 512
KV_WIDTH = 512
ROPE_DIM = 16
ROPE_THETA = 500000.0
IDX_HEADS = 8
IDX_DIM = 64
TOPK_MAX = 256
Q_BLOCK = 128
GMLP_WIDTH = 512
GMLP_GROUPS = 4
GMLP_CHUNK = 128
PEER_HEADS = 8
PEER_NKEYS = 128
PEER_TOPK = 16
EPS = 1e-6

LANE = 128
SUBLANE = 8
INT_MIN = -(2 ** 31)
NEG_BIG = -1e30
VMEM_LIMIT = 56 * 1024 * 1024


def _cp(*sem):
    return pltpu.CompilerParams(dimension_semantics=sem, vmem_limit_bytes=VMEM_LIMIT)


def _dot(a, b):
    return jnp.dot(a, b, preferred_element_type=F32)


def _dot_nt(a, b):
    return lax.dot_general(a, b, (((1,), (1,)), ((), ())), preferred_element_type=F32)


def _norm_linear_kernel(x_ref, g_ref, w_ref, *rest, epilogue):
    x = x_ref[...]
    r = lax.rsqrt(jnp.mean(x * x, axis=-1, keepdims=True) + EPS)
    xn = ((x * r) * g_ref[...]).astype(BF16)
    z = _dot(xn, w_ref[...])
    if epilogue == 'rope':
        cos_ref, sa_ref, sb_ref, o_ref = rest
        n = z.shape[1]
        z = (z * cos_ref[...] + pltpu.roll(z, n - ROPE_DIM // 2, 1) * sa_ref[...]
             + pltpu.roll(z, ROPE_DIM // 2, 1) * sb_ref[...])
    else:
        (o_ref,) = rest
        if epilogue == 'gelu':
            z = jax.nn.gelu(z)
        elif epilogue == 'sigmoid':
            z = jax.nn.sigmoid(z)
    o_ref[...] = z


def norm_linear(x, g, w, epilogue='none', tables=None, tm=512, tn=None):
    n, d = x.shape
    c = w.shape[1]
    tn = min(c, 1024) if tn is None else tn
    tm = min(tm, n)
    in_specs = [pl.BlockSpec((tm, d), lambda i, j: (i, 0)),
                pl.BlockSpec((1, d), lambda i, j: (0, 0)),
                pl.BlockSpec((d, tn), lambda i, j: (0, j))]
    args = [x, g.reshape(1, d), w]
    if epilogue == 'rope':
        nt = tables[0].shape[0] // tm
        for t in tables:
            assert t.shape[1] == tn and t.shape[0] % tm == 0
            in_specs.append(pl.BlockSpec((tm, tn), lambda i, j, nt=nt: (i % nt, 0)))
            args.append(t)
    return pl.pallas_call(
        functools.partial(_norm_linear_kernel, epilogue=epilogue),
        grid=(n // tm, c // tn),
        in_specs=in_specs,
        out_specs=pl.BlockSpec((tm, tn), lambda i, j: (i, j)),
        out_shape=jax.ShapeDtypeStruct((n, c), F32),
        compiler_params=_cp("parallel", "arbitrary"),
        name="norm_linear_" + epilogue,
    )(*args)


def rope_tables(seq_len, pos0, rows, width, rope_lanes):
    half = ROPE_DIM // 2
    inv = ROPE_THETA ** (-jnp.arange(half, dtype=F32) / half)
    pos = (pos0 + (jnp.arange(rows) % seq_len)).astype(F32)
    ang = pos[:, None] * inv[None, :]
    cos, sin = jnp.cos(ang), jnp.sin(ang)
    one = jnp.ones((rows, HEAD_DIM - ROPE_DIM), F32)
    zero8 = jnp.zeros((rows, half), F32)
    zero48 = jnp.zeros((rows, HEAD_DIM - ROPE_DIM), F32)
    c64 = jnp.concatenate([cos, cos, one], axis=1)
    sa64 = jnp.concatenate([-sin, zero8, zero48], axis=1)
    sb64 = jnp.concatenate([zero8, sin, zero48], axis=1)
    reps = rope_lanes // HEAD_DIM
    pad = width - rope_lanes

    def widen(t, fill):
        t = jnp.tile(t, (1, reps))
        if pad:
            t = jnp.concatenate([t, jnp.full((rows, pad), fill, F32)], axis=1)
        return t

    return widen(c64, 1.0), widen(sa64, 0.0), widen(sb64, 0.0)


def _s5_kernel(u_ref, h0r_ref, h0i_ref, ar_ref, ai_ref, bbr_ref, bbi_ref, cr_ref, ci_ref, d_ref,
               y_ref, hr_out, hi_out, bur, bui, hr_s, hi_s, *, bs, tb):
    @pl.when(pl.program_id(0) == 0)
    def _():
        hr_s[...] = h0r_ref[...]
        hi_s[...] = h0i_ref[...]

    u = u_ref[...]
    ub = u.astype(BF16)
    bur[...] = _dot(ub, bbr_ref[...])
    bui[...] = _dot(ub, bbi_ref[...])
    ar = ar_ref[...]
    ai = ai_ref[...]

    if bs >= SUBLANE:
        def group(gi, carry):
            r0 = pl.multiple_of(gi * SUBLANE, SUBLANE)
            hr = hr_s[pl.ds(r0, SUBLANE), :]
            hi = hi_s[pl.ds(r0, SUBLANE), :]
            for l in range(tb):
                row = pl.multiple_of(l * bs + r0, SUBLANE)
                br = bur[pl.ds(row, SUBLANE), :]
                bi = bui[pl.ds(row, SUBLANE), :]
                hr, hi = ar * hr - ai * hi + br, ar * hi + ai * hr + bi
                bur[pl.ds(row, SUBLANE), :] = hr
                bui[pl.ds(row, SUBLANE), :] = hi
            hr_s[pl.ds(r0, SUBLANE), :] = hr
            hi_s[pl.ds(r0, SUBLANE), :] = hi
            return carry

        lax.fori_loop(0, bs // SUBLANE, group, 0)
    else:
        half = SUBLANE // 2
        lo = lax.broadcasted_iota(I32, (SUBLANE, SSM_LANES), 0) < half

        def body(it, carry):
            hr, hi = carry
            row = pl.multiple_of(it * SUBLANE, SUBLANE)
            br = bur[pl.ds(row, SUBLANE), :]
            bi = bui[pl.ds(row, SUBLANE), :]
            xr = ar * hr - ai * hi + br
            xi = ar * hi + ai * hr + bi
            xr2 = pltpu.roll(xr, half, 0)
            xi2 = pltpu.roll(xi, half, 0)
            yr = ar * xr2 - ai * xi2 + br
            yi = ar * xi2 + ai * xr2 + bi
            bur[pl.ds(row, SUBLANE), :] = jnp.where(lo, xr, yr)
            bui[pl.ds(row, SUBLANE), :] = jnp.where(lo, xi, yi)
            return jnp.where(lo, pltpu.roll(yr, half, 0), yr), jnp.where(lo, pltpu.roll(yi, half, 0), yi)

        hr, hi = lax.fori_loop(0, tb * bs // SUBLANE, body, (hr_s[...], hi_s[...]))
        hr_s[...] = hr
        hi_s[...] = hi

    y = _dot(bur[...].astype(BF16), cr_ref[...]) - _dot(bui[...].astype(BF16), ci_ref[...])
    y_ref[...] = y + d_ref[...] * u
    hr_out[...] = hr_s[...]
    hi_out[...] = hi_s[...]


def s5(u_tm, h0r, h0i, sp, bs, tb):
    rows = u_tm.shape[0]
    blk = tb * bs
    srows = max(bs, SUBLANE)
    if bs < SUBLANE:
        assert bs == SUBLANE // 2
        h0r = jnp.concatenate([h0r, h0r], axis=0)
        h0i = jnp.concatenate([h0i, h0i], axis=0)
    const = lambda i: (0, 0)
    y, hr, hi = pl.pallas_call(
        functools.partial(_s5_kernel, bs=bs, tb=tb),
        grid=(rows // blk,),
        in_specs=[pl.BlockSpec((blk, SSM_WIDTH), lambda i: (i, 0)),
                  pl.BlockSpec((srows, SSM_LANES), const),
                  pl.BlockSpec((srows, SSM_LANES), const),
                  pl.BlockSpec((1, SSM_LANES), const),
                  pl.BlockSpec((1, SSM_LANES), const),
                  pl.BlockSpec((SSM_WIDTH, SSM_LANES), const),
                  pl.BlockSpec((SSM_WIDTH, SSM_LANES), const),
                  pl.BlockSpec((SSM_LANES, SSM_WIDTH), const),
                  pl.BlockSpec((SSM_LANES, SSM_WIDTH), const),
                  pl.BlockSpec((1, SSM_WIDTH), const)],
        out_specs=[pl.BlockSpec((blk, SSM_WIDTH), lambda i: (i, 0)),
                   pl.BlockSpec((srows, SSM_LANES), const),
                   pl.BlockSpec((srows, SSM_LANES), const)],
        out_shape=[jax.ShapeDtypeStruct((rows, SSM_WIDTH), F32),
                   jax.ShapeDtypeStruct((srows, SSM_LANES), F32),
                   jax.ShapeDtypeStruct((srows, SSM_LANES), F32)],
        scratch_shapes=[pltpu.VMEM((blk, SSM_LANES), F32), pltpu.VMEM((blk, SSM_LANES), F32),
                        pltpu.VMEM((srows, SSM_LANES), F32), pltpu.VMEM((srows, SSM_LANES), F32)],
        compiler_params=_cp("arbitrary"),
        name="s5",
    )(u_tm, h0r, h0i, sp['abar_re'], sp['abar_im'], sp['bb_re'], sp['bb_im'], sp['c_re'], sp['c_im'], sp['d'])
    return y, hr[srows - bs:], hi[srows - bs:]


def s5_params(a_re, a_im, log_step, b_re, b_im, c_re, c_im, d):
    dt = jnp.exp(log_step)[:, None]
    mag = jnp.exp(dt * a_re)
    abar_re = mag * jnp.cos(dt * a_im)
    abar_im = mag * jnp.sin(dt * a_im)
    den = a_re * a_re + a_im * a_im
    nr = abar_re - 1.0
    coef_re = (nr * a_re + abar_im * a_im) / den
    coef_im = (abar_im * a_re - nr * a_im) / den
    bb_re = coef_re[..., None] * b_re - coef_im[..., None] * b_im
    bb_im = coef_re[..., None] * b_im + coef_im[..., None] * b_re
    eye = jnp.eye(SSM_GROUPS, dtype=F32)

    def bd_in(bb):
        return jnp.einsum('gh,gpc->gchp', eye, bb).reshape(SSM_WIDTH, SSM_LANES).astype(BF16)

    def bd_out(cc):
        return jnp.einsum('gh,gcp->gphc', eye, cc).reshape(SSM_LANES, SSM_WIDTH).astype(BF16)

    return dict(abar_re=abar_re.reshape(1, SSM_LANES), abar_im=abar_im.reshape(1, SSM_LANES),
                bb_re=bd_in(bb_re), bb_im=bd_in(bb_im), c_re=bd_out(c_re), c_im=bd_out(c_im),
                d=d.reshape(1, SSM_WIDTH))


def _sort_key(score):
    bits = lax.bitcast_convert_type(score + 0.0, I32)
    return bits ^ ((bits >> 31) & 0x7FFFFFFF)


def _count_rows(pred_tiles):
    acc = None
    for p in pred_tiles:
        v = jnp.where(p, 1.0, 0.0)
        acc = v if acc is None else acc + v
    return jnp.sum(acc, axis=1, keepdims=True)


def _kth_threshold(key_ref, ntiles, rows, topk):
    def body(i, t):
        cand = t + lax.shift_left(jnp.int32(1), 31 - i)
        cb = jnp.broadcast_to(cand, (rows, LANE))
        cnt = _count_rows([key_ref[:, c * LANE:(c + 1) * LANE] >= cb for c in range(ntiles)])
        return jnp.where(cnt >= topk, cand, t)

    return lax.fori_loop(0, 32, body, jnp.full((rows, 1), INT_MIN, I32))


def _tie_cut(key_ref, kpos_tiles, thr, need, ntiles, rows, nbits):
    tb = jnp.broadcast_to(thr, (rows, LANE))

    def body(i, m):
        cand = m + lax.shift_left(jnp.int32(1), nbits - 1 - i)
        cb = jnp.broadcast_to(cand, (rows, LANE))
        cnt = _count_rows([(key_ref[:, c * LANE:(c + 1) * LANE] == tb) & (kpos_tiles[c] < cb)
                           for c in range(ntiles)])
        return jnp.where(cnt < need, cand, m)

    return lax.fori_loop(0, nbits, body, jnp.zeros((rows, 1), I32))


def _select_mask(key_ref, mask_ref, cut_ref, kpos_of_tile, ntiles, rows, topk, nbits):
    thr = _kth_threshold(key_ref, ntiles, rows, topk)
    tb = jnp.broadcast_to(thr, (rows, LANE))
    tiles = [key_ref[:, c * LANE:(c + 1) * LANE] for c in range(ntiles)]
    n_gt = _count_rows([t > tb for t in tiles])
    n_eq = _count_rows([t == tb for t in tiles])
    need = topk - n_gt
    cut_ref[...] = jnp.full((rows, LANE), 2 ** 30, I32)
    excess = jnp.max(jnp.where(thr > INT_MIN, n_eq - need, 0.0)) > 0.0

    @pl.when(excess)
    def _():
        kpos_tiles = [kpos_of_tile(c) for c in range(ntiles)]
        m = _tie_cut(key_ref, kpos_tiles, thr, need, ntiles, rows, nbits)
        cut_ref[...] = jnp.broadcast_to(m, (rows, LANE))

    cut = cut_ref[...]
    for c in range(ntiles):
        t = key_ref[:, c * LANE:(c + 1) * LANE]
        sel = (t > tb) | ((t == tb) & (t > INT_MIN) & (kpos_of_tile(c) <= cut))
        mask_ref[:, c * LANE:(c + 1) * LANE] = jnp.where(sel, 0.0, NEG_BIG)


def _dsa_prompt_kernel(q_ref, qi_ref, kiwi_q_ref, k_ref, v_ref, kiwi_ref, o_ref,
                       key_ref, mask_ref, s_ref, cut_ref, *, lk, topk, qb0, kc):
    rows = Q_BLOCK
    qb = pl.program_id(1) + qb0
    nchunk = lk // kc
    ntiles = lk // LANE
    lane = lax.broadcasted_iota(I32, (rows, LANE), 1)
    low = lane < HEAD_DIM
    qpos = qb * rows + lax.broadcasted_iota(I32, (rows, kc), 0)

    wscale = (IDX_HEADS ** -0.5) * (IDX_DIM ** -0.5)
    wq = kiwi_q_ref[0]
    qi_parts = []
    for p in range(IDX_HEADS // 2):
        pair = qi_ref[0, :, p * LANE:(p + 1) * LANE]
        qi_parts.append(jnp.where(low, pair, 0.0).astype(BF16))
        qi_parts.append(jnp.where(low, pltpu.roll(pair, HEAD_DIM, 1), 0.0).astype(BF16))
    wcols = [wq[:, IDX_DIM + h:IDX_DIM + h + 1] * wscale for h in range(IDX_HEADS)]
    for c in range(nchunk):
        kic = kiwi_ref[0, c * kc:(c + 1) * kc, :]
        acc = jnp.zeros((rows, kc), F32)
        for h in range(IDX_HEADS):
            acc = acc + jnp.maximum(_dot_nt(qi_parts[h], kic), 0.0) * wcols[h]
        kpos = c * kc + lax.broadcasted_iota(I32, (rows, kc), 1)
        key_ref[:, c * kc:(c + 1) * kc] = jnp.where(kpos <= qpos, _sort_key(acc), INT_MIN)

    def kpos_of_tile(c):
        return c * LANE + lane

    _select_mask(key_ref, mask_ref, cut_ref, kpos_of_tile, ntiles, rows, topk, int(math.log2(lk)) + 1)

    scale = HEAD_DIM ** -0.5
    for p in range(N_HEADS // 2):
        qpair = q_ref[0, :, p * LANE:(p + 1) * LANE]
        outs = []
        for par in range(2):
            qh = jnp.where(low if par == 0 else ~low, qpair, 0.0).astype(BF16)
            m = jnp.full((rows, 1), NEG_BIG, F32)
            for c in range(nchunk):
                s = _dot_nt(qh, k_ref[0, c * kc:(c + 1) * kc, p * LANE:(p + 1) * LANE]) * scale
                s = s + mask_ref[:, c * kc:(c + 1) * kc]
                s_ref[:, c * kc:(c + 1) * kc] = s
                m = jnp.maximum(m, jnp.max(s, axis=1, keepdims=True))
            l = jnp.zeros((rows, 1), F32)
            o = jnp.zeros((rows, LANE), F32)
            for c in range(nchunk):
                pr = jnp.exp(s_ref[:, c * kc:(c + 1) * kc] - m)
                l = l + jnp.sum(pr, axis=1, keepdims=True)
                o = o + _dot(pr.astype(BF16), v_ref[0, c * kc:(c + 1) * kc, p * LANE:(p + 1) * LANE])
            outs.append(o / l)
        o_ref[0, :, p * LANE:(p + 1) * LANE] = jnp.where(low, outs[0], outs[1])


def dsa_prompt(q, qi, kiwi, k_bf, v_bf, kiwi_bf, seq_len):
    bsz = q.shape[0]
    topk = min(TOPK_MAX, seq_len // 4)
    nqb = seq_len // Q_BLOCK
    kc = min(512, seq_len)
    cands = sorted({min(seq_len, kc * 2 ** j) for j in range(12)} | {seq_len * 3 // 4 // kc * kc or kc, seq_len})
    buckets = {}
    for qb in range(nqb):
        lk = next(c for c in cands if c >= (qb + 1) * Q_BLOCK)
        buckets.setdefault(lk, []).append(qb)
    outs = []
    for lk, qbs in sorted(buckets.items()):
        qb0, n = qbs[0], len(qbs)
        qmap = lambda b, j, qb0=qb0: (b, qb0 + j, 0)
        kmap = lambda b, j: (b, 0, 0)
        outs.append(pl.pallas_call(
            functools.partial(_dsa_prompt_kernel, lk=lk, topk=topk, qb0=qb0, kc=kc),
            grid=(bsz, n),
            in_specs=[pl.BlockSpec((1, Q_BLOCK, ATTN_WIDTH), qmap),
                      pl.BlockSpec((1, Q_BLOCK, ATTN_WIDTH), qmap),
                      pl.BlockSpec((1, Q_BLOCK, LANE), qmap),
                      pl.BlockSpec((1, lk, KV_WIDTH), kmap),
                      pl.BlockSpec((1, lk, KV_WIDTH), kmap),
                      pl.BlockSpec((1, lk, LANE), kmap)],
            out_specs=pl.BlockSpec((1, Q_BLOCK, ATTN_WIDTH), lambda b, j: (b, j, 0)),
            out_shape=jax.ShapeDtypeStruct((bsz, n * Q_BLOCK, ATTN_WIDTH), F32),
            scratch_shapes=[pltpu.VMEM((Q_BLOCK, lk), I32), pltpu.VMEM((Q_BLOCK, lk), F32),
                            pltpu.VMEM((Q_BLOCK, lk), F32), pltpu.VMEM((Q_BLOCK, LANE), I32)],
            compiler_params=_cp("parallel", "arbitrary"),
            name=f"dsa_prompt_{lk}",
        )(q, qi, kiwi, k_bf, v_bf, kiwi_bf))
    return jnp.concatenate(outs, axis=1)


def _dsa_sample_kernel(pt_ref, q_ref, qi_ref, kiwi_ref, kn_ref, vn_ref, *rest, n_pages, t_new, topk):
    kidx_refs = rest[:n_pages]
    k_refs = rest[n_pages:2 * n_pages]
    v_refs = rest[2 * n_pages:3 * n_pages]
    o_ref, key_ref, mask_ref, s_ref, cut_ref = rest[3 * n_pages:]
    del pt_ref
    rows = N_HEADS * t_new
    ntiles = n_pages + 1
    lane = lax.broadcasted_iota(I32, (t_new, LANE), 1)
    low = lane < HEAD_DIM

    kiwi = kiwi_ref[...]
    wscale = (IDX_HEADS ** -0.5) * (IDX_DIM ** -0.5)
    parts, wparts = [], []
    for p in range(IDX_HEADS // 2):
        pair = qi_ref[:, p * LANE:(p + 1) * LANE]
        parts.append(pair[:, :IDX_DIM])
        parts.append(pltpu.roll(pair, HEAD_DIM, 1)[:, :IDX_DIM])
    for h in range(IDX_HEADS):
        wparts.append(kiwi[:, IDX_DIM + h:IDX_DIM + h + 1] * wscale)
    qi_r = jnp.concatenate(parts, axis=0).astype(BF16)
    wcol = jnp.concatenate(wparts, axis=0)
    zpad = jnp.zeros((PAGE_SIZE - t_new, IDX_DIM), F32)
    ki_new = jnp.concatenate([kiwi[:, :IDX_DIM], zpad], axis=0)

    trow = lax.broadcasted_iota(I32, (t_new, LANE), 0)
    for c in range(ntiles):
        keys = kidx_refs[c][...] if c < n_pages else ki_new
        s = jnp.maximum(_dot_nt(qi_r, keys.astype(BF16)), 0.0) * wcol
        sc = s[0:t_new]
        for h in range(1, IDX_HEADS):
            sc = sc + s[h * t_new:(h + 1) * t_new]
        k = _sort_key(sc)
        if c == n_pages:
            k = jnp.where(lane <= trow, k, INT_MIN)
        key_ref[:, c * LANE:(c + 1) * LANE] = k

    def kpos_of_tile(c):
        return c * LANE + lane

    _select_mask(key_ref, mask_ref, cut_ref, kpos_of_tile, ntiles, t_new, topk,
                 int(math.log2(ntiles * LANE)) + 1)

    q = q_ref[...]
    qt = jnp.concatenate([q] * N_HEADS, axis=0)
    rhead = lax.broadcasted_iota(I32, (rows, KV_WIDTH), 0) // t_new
    lhead = lax.broadcasted_iota(I32, (rows, KV_WIDTH), 1) // HEAD_DIM
    own = rhead == lhead
    qbd = jnp.where(own, qt, 0.0).astype(BF16)
    scale = HEAD_DIM ** -0.5
    zkv = jnp.zeros((PAGE_SIZE - t_new, KV_WIDTH), F32)
    k_new = jnp.concatenate([kn_ref[...], zkv], axis=0).astype(BF16)
    v_new = jnp.concatenate([vn_ref[...], zkv], axis=0).astype(BF16)
    m = jnp.full((rows, 1), NEG_BIG, F32)
    for c in range(ntiles):
        kt = k_refs[c][...].astype(BF16) if c < n_pages else k_new
        mk = mask_ref[:, c * LANE:(c + 1) * LANE]
        s = _dot_nt(qbd, kt) * scale + jnp.concatenate([mk] * N_HEADS, axis=0)
        s_ref[:, c * LANE:(c + 1) * LANE] = s
        m = jnp.maximum(m, jnp.max(s, axis=1, keepdims=True))
    l = jnp.zeros((rows, 1), F32)
    o = jnp.zeros((rows, KV_WIDTH), F32)
    for c in range(ntiles):
        vt = v_refs[c][...].astype(BF16) if c < n_pages else v_new
        pr = jnp.exp(s_ref[:, c * LANE:(c + 1) * LANE] - m)
        l = l + jnp.sum(pr, axis=1, keepdims=True)
        o = o + _dot(pr.astype(BF16), vt)
    o = jnp.where(own, o / l, 0.0)
    out = o[0:t_new]
    for h in range(1, N_HEADS):
        out = out + o[h * t_new:(h + 1) * t_new]
    o_ref[...] = out


def dsa_sample(q, qi, kiwi, k_new, v_new, cache_k, cache_v, cache_kidx, pages, t_new):
    bd, n_pages = pages.shape
    seq_len = n_pages * PAGE_SIZE + t_new
    topk = min(TOPK_MAX, seq_len // 4)
    rows = N_HEADS * t_new
    nl = (n_pages + 1) * LANE
    tok = lambda w: pl.BlockSpec((t_new, w), lambda b, pt: (b, 0))

    def page_specs(width):
        return [pl.BlockSpec((None, PAGE_SIZE, width), lambda b, pt, p=p: (pt[b, p], 0, 0)) for p in range(n_pages)]

    grid_spec = pltpu.PrefetchScalarGridSpec(
        num_scalar_prefetch=1,
        grid=(bd,),
        in_specs=[tok(ATTN_WIDTH), tok(ATTN_WIDTH), tok(LANE), tok(KV_WIDTH), tok(KV_WIDTH)]
        + page_specs(IDX_DIM) + page_specs(KV_WIDTH) + page_specs(KV_WIDTH),
        out_specs=pl.BlockSpec((t_new, ATTN_WIDTH), lambda b, pt: (b, 0)),
        scratch_shapes=[pltpu.VMEM((t_new, nl), I32), pltpu.VMEM((t_new, nl), F32),
                        pltpu.VMEM((rows, nl), F32), pltpu.VMEM((t_new, LANE), I32)],
    )
    return pl.pallas_call(
        functools.partial(_dsa_sample_kernel, n_pages=n_pages, t_new=t_new, topk=topk),
        grid_spec=grid_spec,
        out_shape=jax.ShapeDtypeStruct((bd * t_new, ATTN_WIDTH), F32),
        compiler_params=_cp("arbitrary"),
        name="dsa_sample",
    )(pages, q, qi, kiwi, k_new, v_new, *([cache_kidx] * n_pages), *([cache_k] * n_pages), *([cache_v] * n_pages))


def _gmlp_kernel(z_ref, g_ref, b_ref, w_ref, bias_ref, um_ref, vn_ref, *, nchunk):
    for c in range(nchunk):
        rs = slice(c * GMLP_CHUNK, (c + 1) * GMLP_CHUNK)
        u = z_ref[rs, :GMLP_WIDTH]
        v = z_ref[rs, GMLP_WIDTH:]
        mu = jnp.mean(v, axis=-1, keepdims=True)
        var = jnp.mean(jnp.square(v - mu), axis=-1, keepdims=True)
        vn = ((v - mu) * lax.rsqrt(var + EPS)) * g_ref[...] + b_ref[...]
        vn_ref[rs, :] = vn
        vb = vn.astype(BF16)
        gw = GMLP_WIDTH // GMLP_GROUPS
        for g in range(GMLP_GROUPS):
            mixed = _dot(w_ref[g], vb[:, g * gw:(g + 1) * gw]) + bias_ref[:, g * gw:(g + 1) * gw]
            um_ref[rs, g * gw:(g + 1) * gw] = u[:, g * gw:(g + 1) * gw] * mixed


def gmlp(z, ln_g, ln_b, w_mix, bias_full, tm=512):
    n = z.shape[0]
    tm = min(tm, n)
    const2 = lambda i: (0, 0)
    return pl.pallas_call(
        functools.partial(_gmlp_kernel, nchunk=tm // GMLP_CHUNK),
        grid=(n // tm,),
        in_specs=[pl.BlockSpec((tm, 2 * GMLP_WIDTH), lambda i: (i, 0)),
                  pl.BlockSpec((1, GMLP_WIDTH), const2),
                  pl.BlockSpec((1, GMLP_WIDTH), const2),
                  pl.BlockSpec((GMLP_GROUPS, GMLP_CHUNK, GMLP_CHUNK), lambda i: (0, 0, 0)),
                  pl.BlockSpec((GMLP_CHUNK, GMLP_WIDTH), const2)],
        out_specs=[pl.BlockSpec((tm, GMLP_WIDTH), lambda i: (i, 0)),
                   pl.BlockSpec((tm, GMLP_WIDTH), lambda i: (i, 0))],
        out_shape=[jax.ShapeDtypeStruct((n, GMLP_WIDTH), F32), jax.ShapeDtypeStruct((n, GMLP_WIDTH), F32)],
        compiler_params=_cp("parallel"),
        name="gmlp",
    )(z, ln_g.reshape(1, -1), ln_b.reshape(1, -1), w_mix, bias_full)


def gmlp_mix_params(ws, bs, seq_len):
    c = min(GMLP_CHUNK, seq_len)
    w = jnp.where(jnp.tril(jnp.ones((c, c), dtype=bool))[None], ws[:, :c, :c], 0.0)
    reps = GMLP_CHUNK // c
    if reps > 1:
        w = jnp.einsum('ab,gts->gatbs', jnp.eye(reps, dtype=F32), w).reshape(GMLP_GROUPS, GMLP_CHUNK, GMLP_CHUNK)
    bias = jnp.tile(bs[:, :c], (1, reps))
    bias_full = jnp.repeat(bias.T, GMLP_WIDTH // GMLP_GROUPS, axis=1)
    return w.astype(BF16), bias_full


def _merge_kernel(x_ref, y_ref, attn_ref, um_ref, gates_ref, wglu_ref, wao_ref, wgo_ref, wout_ref, o_ref):
    ga = _dot(jax.nn.gelu(y_ref[...]).astype(BF16), wglu_ref[...])
    br_a = ga[:, :D_MODEL] * jax.nn.sigmoid(ga[:, D_MODEL:])
    br_b = _dot(attn_ref[...].astype(BF16), wao_ref[...])
    br_c = _dot(um_ref[...].astype(BF16), wgo_ref[...])
    merged = (gates_ref[:, :D_MODEL] * br_a + gates_ref[:, D_MODEL:2 * D_MODEL] * br_b
              + gates_ref[:, 2 * D_MODEL:] * br_c)
    o_ref[...] = x_ref[...] + _dot(merged.astype(BF16), wout_ref[...])


def merge(x, y_ssm, attn, um, gates, w_glu, w_attn_o, w_gmlp_o, w_out, tm=256):
    n = x.shape[0]
    tm = min(tm, n)
    row = lambda w: pl.BlockSpec((tm, w), lambda i: (i, 0))
    full = lambda a: pl.BlockSpec(a.shape, lambda i: (0, 0))
    return pl.pallas_call(
        _merge_kernel,
        grid=(n // tm,),
        in_specs=[row(D_MODEL), row(SSM_WIDTH), row(ATTN_WIDTH), row(GMLP_WIDTH), row(3 * D_MODEL),
                  full(w_glu), full(w_attn_o), full(w_gmlp_o), full(w_out)],
        out_specs=row(D_MODEL),
        out_shape=jax.ShapeDtypeStruct((n, D_MODEL), F32),
        compiler_params=_cp("parallel"),
        name="merge",
    )(x, y_ssm, attn, um, gates, w_glu, w_attn_o, w_gmlp_o, w_out)


def _peer_q_kernel(x_ref, g_ref, wq_ref, keys_ref, st_ref, xnt_ref):
    x = x_ref[...]
    r = lax.rsqrt(jnp.mean(x * x, axis=-1, keepdims=True) + EPS)
    xn = (x * r) * g_ref[...]
    xnt_ref[...] = xn.T.astype(BF16)
    q = _dot(xn.astype(BF16), wq_ref[...]).astype(BF16)
    for hc in range(2 * PEER_HEADS):
        st_ref[hc] = _dot_nt(keys_ref[hc], q[:, hc * LANE:(hc + 1) * LANE])


def peer_q(x, g, wq, keys, tm=256):
    n = x.shape[0]
    tm = min(tm, n)
    nhc = 2 * PEER_HEADS
    return pl.pallas_call(
        _peer_q_kernel,
        grid=(n // tm,),
        in_specs=[pl.BlockSpec((tm, D_MODEL), lambda i: (i, 0)),
                  pl.BlockSpec((1, D_MODEL), lambda i: (0, 0)),
                  pl.BlockSpec(wq.shape, lambda i: (0, 0)),
                  pl.BlockSpec(keys.shape, lambda i: (0, 0, 0))],
        out_specs=[pl.BlockSpec((nhc, PEER_NKEYS, tm), lambda i: (0, 0, i)),
                   pl.BlockSpec((D_MODEL, tm), lambda i: (0, i))],
        out_shape=[jax.ShapeDtypeStruct((nhc, PEER_NKEYS, n), F32), jax.ShapeDtypeStruct((D_MODEL, n), BF16)],
        compiler_params=_cp("parallel"),
        name="peer_q",
    )(x, g.reshape(1, -1), wq, keys)


def _top_values(s, count):
    vals = []
    for _ in range(count):
        m = jnp.max(s, axis=0, keepdims=True)
        vals.append(m)
        s = jnp.where(s == m, -jnp.inf, s)
    return vals


def _peer_select_kernel(st_ref, c_ref, e0_ref, e1_ref):
    def head(h, carry):
        s0 = st_ref[2 * h]
        s1 = st_ref[2 * h + 1]
        a = _top_values(s0, PEER_TOPK)
        b = jnp.concatenate(_top_values(s1, PEER_TOPK), axis=0)
        cand = jnp.concatenate([ar + b for ar in a], axis=0)
        top = _top_values(cand, PEER_TOPK + 1)
        thr = 0.5 * (top[PEER_TOPK - 1] + top[PEER_TOPK])
        a0, b0 = a[0], b[0:1]
        z = jnp.sum(jnp.where(cand >= thr, jnp.exp(cand - (a0 + b0)), 0.0), axis=0, keepdims=True)
        c_ref[h] = thr - s0
        e0_ref[h] = jnp.exp(s0 - a0) / z
        e1_ref[h] = jnp.exp(s1 - b0)
        return carry

    lax.fori_loop(0, PEER_HEADS, head, 0)


def peer_select(st, tm=128):
    n = st.shape[2]
    spec = pl.BlockSpec((PEER_HEADS, PEER_NKEYS, tm), lambda i: (0, 0, i))
    shp = jax.ShapeDtypeStruct((PEER_HEADS, PEER_NKEYS, n), F32)
    return pl.pallas_call(
        _peer_select_kernel,
        grid=(n // tm,),
        in_specs=[pl.BlockSpec((2 * PEER_HEADS, PEER_NKEYS, tm), lambda i: (0, 0, i))],
        out_specs=[spec, spec, spec],
        out_shape=[shp, shp, shp],
        compiler_params=_cp("parallel"),
        name="peer_select",
    )(st)


def _peer_main_kernel(x_ref, xnt_ref, u_ref, vt_ref, s1_ref, e1_ref, c_ref, e0_ref, o_ref, acc_ref, h_ref, *, ib):
    ci = pl.program_id(1)

    @pl.when(ci == 0)
    def _():
        acc_ref[...] = jnp.zeros_like(acc_ref)

    act = jax.nn.gelu(_dot(u_ref[...], xnt_ref[...]))
    for ii in range(ib):
        w = None
        for h in range(PEER_HEADS):
            term = jnp.where(s1_ref[h, 0] >= c_ref[h, ii:ii + 1, :], e1_ref[h], 0.0) * e0_ref[h, ii:ii + 1, :]
            w = term if w is None else w + term
        h_ref[ii * PEER_NKEYS:(ii + 1) * PEER_NKEYS, :] = (
            w * act[ii * PEER_NKEYS:(ii + 1) * PEER_NKEYS, :]).astype(BF16)
    acc_ref[...] += _dot(vt_ref[...], h_ref[...])

    @pl.when(ci == pl.num_programs(1) - 1)
    def _():
        o_ref[...] = x_ref[...] + acc_ref[...].T


def peer_main(x, xnt, u_tab, vt_tab, st, cthr, e0, e1, tm=512, ib=8):
    n = x.shape[0]
    tm = min(tm, n)
    ec = ib * PEER_NKEYS
    nchunk = u_tab.shape[0] // ec
    st4 = st.reshape(PEER_HEADS, 2, PEER_NKEYS, n)
    return pl.pallas_call(
        functools.partial(_peer_main_kernel, ib=ib),
        grid=(n // tm, nchunk),
        in_specs=[pl.BlockSpec((tm, D_MODEL), lambda t, c: (t, 0)),
                  pl.BlockSpec((D_MODEL, tm), lambda t, c: (0, t)),
                  pl.BlockSpec((ec, D_MODEL), lambda t, c: (c, 0)),
                  pl.BlockSpec((D_MODEL, ec), lambda t, c: (0, c)),
                  pl.BlockSpec((PEER_HEADS, 1, PEER_NKEYS, tm), lambda t, c: (0, 1, 0, t)),
                  pl.BlockSpec((PEER_HEADS, PEER_NKEYS, tm), lambda t, c: (0, 0, t)),
                  pl.BlockSpec((PEER_HEADS, ib, tm), lambda t, c: (0, c, t)),
                  pl.BlockSpec((PEER_HEADS, ib, tm), lambda t, c: (0, c, t))],
        out_specs=pl.BlockSpec((tm, D_MODEL), lambda t, c: (t, 0)),
        out_shape=jax.ShapeDtypeStruct((n, D_MODEL), F32),
        scratch_shapes=[pltpu.VMEM((D_MODEL, tm), F32), pltpu.VMEM((ec, tm), BF16)],
        compiler_params=_cp("parallel", "arbitrary"),
        name="peer_main",
    )(x, xnt, u_tab, vt_tab, st4, e1, cthr, e0)


def peer(x, g, wq, keys, u_tab, vt_tab):
    st, xnt = peer_q(x, g, wq, keys)
    cthr, e0, e1 = peer_select(st)
    return peer_main(x, xnt, u_tab, vt_tab, st, cthr, e0, e1)


def _rmsnorm_kernel(x_ref, g_ref, o_ref):
    x = x_ref[...]
    r = lax.rsqrt(jnp.mean(x * x, axis=-1, keepdims=True) + EPS)
    o_ref[...] = (x * r) * g_ref[...]


def rmsnorm(x, g, tm=512):
    n, d = x.shape
    tm = min(tm, n)
    return pl.pallas_call(
        _rmsnorm_kernel,
        grid=(n // tm,),
        in_specs=[pl.BlockSpec((tm, d), lambda i: (i, 0)), pl.BlockSpec((1, d), lambda i: (0, 0))],
        out_specs=pl.BlockSpec((tm, d), lambda i: (i, 0)),
        out_shape=jax.ShapeDtypeStruct((n, d), F32),
        compiler_params=_cp("parallel"),
        name="rmsnorm",
    )(x, g.reshape(1, d))


def layer_params(norm_mix, w_in, ssm, w_glu, w_attn_o, gmlp_ln_g, gmlp_ln_b, gmlp_ws, gmlp_bs, w_gmlp_o,
                 w_out, norm_ffn, peer_wq, peer_keys, peer_u, peer_v):
    sizes = (SSM_WIDTH, ATTN_WIDTH, KV_WIDTH, KV_WIDTH, IDX_HEADS * IDX_DIM, IDX_DIM, IDX_HEADS,
             2 * GMLP_WIDTH, 3 * D_MODEL)
    pts = np.cumsum(sizes)[:-1]
    w_u, w_q, w_k, w_v, w_qi, w_ki, w_wi, w_zg, w_gates = jnp.split(w_in.astype(BF16), pts, axis=1)
    w_kiwi = jnp.concatenate([w_ki, w_wi, jnp.zeros((D_MODEL, LANE - IDX_DIM - IDX_HEADS), BF16)], axis=1)
    return dict(norm_mix=norm_mix, w_u=w_u, w_q=w_q, w_k=w_k, w_v=w_v, w_qi=w_qi, w_kiwi=w_kiwi, w_zg=w_zg,
                w_gates=w_gates, ssm=s5_params(*ssm), w_glu=w_glu.astype(BF16), w_attn_o=w_attn_o.astype(BF16),
                gmlp_ln_g=gmlp_ln_g, gmlp_ln_b=gmlp_ln_b, gmlp_ws=gmlp_ws, gmlp_bs=gmlp_bs,
                w_gmlp_o=w_gmlp_o.astype(BF16), w_out=w_out.astype(BF16), norm_ffn=norm_ffn,
                peer_wq=peer_wq.astype(BF16),
                peer_keys=peer_keys.reshape(2 * PEER_HEADS, PEER_NKEYS, LANE).astype(BF16),
                peer_u=peer_u.astype(BF16), peer_vt=peer_v.T.astype(BF16))


def trunk_layer(x, pos0, h0r, h0i, p, attend):
    bsz, seq_len, _ = x.shape
    n = bsz * seq_len
    xf = x.reshape(n, D_MODEL)
    rows = max(seq_len, min(512, n))
    t512 = rope_tables(seq_len, pos0, rows, ATTN_WIDTH, ATTN_WIDTH)
    t128 = rope_tables(seq_len, pos0, rows, LANE, IDX_DIM)
    nm = p['norm_mix']
    u = norm_linear(xf, nm, p['w_u'])
    q = norm_linear(xf, nm, p['w_q'], 'rope', t512)
    k = norm_linear(xf, nm, p['w_k'], 'rope', t512)
    v = norm_linear(xf, nm, p['w_v'])
    qi = norm_linear(xf, nm, p['w_qi'], 'rope', t512)
    kiwi = norm_linear(xf, nm, p['w_kiwi'], 'rope', t128)
    zg = norm_linear(xf, nm, p['w_zg'], 'gelu')
    gates = norm_linear(xf, nm, p['w_gates'], 'sigmoid')

    u_tm = u.reshape(bsz, seq_len, SSM_WIDTH).transpose(1, 0, 2).reshape(n, SSM_WIDTH)
    tb = min(seq_len, 128)
    y_tm, hr, hi = s5(u_tm, h0r, h0i, p['ssm'], bsz, tb)
    y_ssm = y_tm.reshape(seq_len, bsz, SSM_WIDTH).transpose(1, 0, 2).reshape(n, SSM_WIDTH)

    attn = attend(q, qi, kiwi, k, v)

    w_mix, bias_full = gmlp_mix_params(p['gmlp_ws'], p['gmlp_bs'], seq_len)
    um, vn = gmlp(zg, p['gmlp_ln_g'], p['gmlp_ln_b'], w_mix, bias_full)

    x1 = merge(xf, y_ssm, attn, um, gates, p['w_glu'], p['w_attn_o'], p['w_gmlp_o'], p['w_out'])
    x2 = peer(x1, p['norm_ffn'], p['peer_wq'], p['peer_keys'], p['peer_u'], p['peer_vt'])
    return x2.reshape(bsz, seq_len, D_MODEL), k, v, kiwi[:, :IDX_DIM], hr, hi, vn


def kernel(x_prompt, x_sample, cache_k, cache_v, cache_kidx, page_table, state_ssm_re, state_ssm_im, norm_mix, w_in, ssm_a_re, ssm_a_im, ssm_log_step, ssm_b_re, ssm_b_im, ssm_c_re, ssm_c_im, ssm_d, w_glu, w_attn_o, gmlp_ln_g, gmlp_ln_b, gmlp_ws, gmlp_bs, w_gmlp_o, w_out, norm_ffn, peer_wq, peer_keys, peer_u, peer_v, norm_final):
    depth = w_in.shape[0]
    bsz, seq, _ = x_prompt.shape
    bd, t_new, _ = x_sample.shape
    n_pool = cache_k.shape[1]
    ck = cache_k.reshape(depth * n_pool, PAGE_SIZE, KV_WIDTH)
    cv = cache_v.reshape(depth * n_pool, PAGE_SIZE, KV_WIDTH)
    cki = cache_kidx.reshape(depth * n_pool, PAGE_SIZE, IDX_DIM)
    zeros = jnp.zeros((bsz, SSM_LANES), F32)
    xp, xs = x_prompt, x_sample
    outs_p, outs_s = [], []
    for l in range(depth):
        p = layer_params(norm_mix[l], w_in[l],
                         (ssm_a_re[l], ssm_a_im[l], ssm_log_step[l], ssm_b_re[l], ssm_b_im[l], ssm_c_re[l],
                          ssm_c_im[l], ssm_d[l]),
                         w_glu[l], w_attn_o[l], gmlp_ln_g[l], gmlp_ln_b[l], gmlp_ws[l], gmlp_bs[l], w_gmlp_o[l],
                         w_out[l], norm_ffn[l], peer_wq[l], peer_keys[l], peer_u[l], peer_v[l])

        def attend_p(q, qi, kiwi, k, v):
            shp = lambda a: a.reshape(bsz, seq, a.shape[-1])
            o = dsa_prompt(shp(q), shp(qi), shp(kiwi), shp(k).astype(BF16), shp(v).astype(BF16),
                           shp(kiwi).astype(BF16), seq)
            return o.reshape(bsz * seq, ATTN_WIDTH)

        def attend_s(q, qi, kiwi, k, v, l=l):
            return dsa_sample(q, qi, kiwi, k, v, ck, cv, cki, page_table + l * n_pool, t_new)

        xp, k1, v1, ki1, hr1, hi1, _ = trunk_layer(xp, 0, zeros, zeros, p, attend_p)
        outs_p.append((k1.reshape(bsz, seq, N_HEADS, HEAD_DIM), v1.reshape(bsz, seq, N_HEADS, HEAD_DIM),
                       ki1.reshape(bsz, seq, IDX_DIM), hr1.reshape(bsz, SSM_GROUPS, SSM_STATE),
                       hi1.reshape(bsz, SSM_GROUPS, SSM_STATE)))
        xs, k2, v2, ki2, hr2, hi2, gv2 = trunk_layer(
            xs, PAST_LEN, state_ssm_re[l].reshape(bd, SSM_LANES), state_ssm_im[l].reshape(bd, SSM_LANES), p, attend_s)
        outs_s.append((k2.reshape(bd, t_new, N_HEADS, HEAD_DIM), v2.reshape(bd, t_new, N_HEADS, HEAD_DIM),
                       ki2.reshape(bd, t_new, IDX_DIM), hr2.reshape(bd, SSM_GROUPS, SSM_STATE),
                       hi2.reshape(bd, SSM_GROUPS, SSM_STATE), gv2.reshape(bd, t_new, GMLP_WIDTH)))
    y_prompt = rmsnorm(xp.reshape(bsz * seq, D_MODEL), norm_final).reshape(bsz, seq, D_MODEL)
    y_sample = rmsnorm(xs.reshape(bd * t_new, D_MODEL), norm_final).reshape(bd, t_new, D_MODEL)
    stack = lambda outs, i: jnp.stack([o[i] for o in outs])
    return (y_prompt, y_sample,
            stack(outs_p, 0), stack(outs_p, 1), stack(outs_p, 2), stack(outs_p, 3), stack(outs_p, 4),
            stack(outs_s, 0), stack(outs_s, 1), stack(outs_s, 2), stack(outs_s, 3), stack(outs_s, 4),
            stack(outs_s, 5))
```
